```python
import jax, jax.numpy as jnp
from jax import lax
import numpy as np

D_MODEL = 2048
BATCH = 16
SEQ = 2048
DEPTH = 4
DEC_BATCH = 4
DEC_SEQ = 8192
PAST_LEN = 128

GRID_W = 64
N_BRANCH = 4
BRANCH_W = D_MODEL // N_BRANCH
N_GROUPS = 4
GROUP_W = BRANCH_W // N_GROUPS
POOL_WINDOWS = (2, 4, 8, 16)
HEAD_DIM = 128
N_Q_HEADS = BRANCH_W // HEAD_DIM
N_KV_HEADS = 2
Q_PER_KV = N_Q_HEADS // N_KV_HEADS
KV_W = N_KV_HEADS * HEAD_DIM
ATTN_IN_W = BRANCH_W + 2 * KV_W
IN_W = 2 * BRANCH_W + 2 * ATTN_IN_W
WINDOW = 128
BLOCK = 128
ROPE_THETA = 10000.0
D_FF = ((8 * D_MODEL // 3 + 255) // 256) * 256
EPS = 1e-6
NEG = -1e30

kernel_name = 'hybrid_gated_parallel_encoder'


def rms_norm(x, g):
    xf = x.astype(jnp.float32)
    y = xf * lax.rsqrt(jnp.mean(xf * xf, axis=-1, keepdims=True) + EPS)
    return (y * g.astype(jnp.float32)).astype(x.dtype)


def swiglu(x, w_gate, w_up, w_down):
    return (jax.nn.silu(x @ w_gate) * (x @ w_up)) @ w_down


def rope_tables(pos, dim):
    inv = ROPE_THETA ** (-jnp.arange(0, dim, 2, dtype=jnp.float32) / dim)
    ang = pos.astype(jnp.float32)[:, None] * inv[None, :]
    ang = jnp.concatenate([ang, ang], axis=-1)
    return jnp.cos(ang), jnp.sin(ang)


def apply_rope(x, cos, sin):
    xf = x.astype(jnp.float32)
    x1, x2 = jnp.split(xf, 2, axis=-1)
    rot = jnp.concatenate([-x2, x1], axis=-1)
    return (xf * cos[None, :, None, :] + rot * sin[None, :, None, :]).astype(x.dtype)


def apply_axial_rope(x, rope_row, rope_col):
    half = HEAD_DIM // 2
    return jnp.concatenate([apply_rope(x[..., :half], *rope_row),
                            apply_rope(x[..., half:], *rope_col)], axis=-1)


def position_tables(S):
    rows = S // GRID_W
    t = jnp.arange(S, dtype=jnp.int32)
    row = jnp.repeat(jnp.arange(rows, dtype=jnp.int32), GRID_W)
    col = jnp.tile(jnp.arange(GRID_W, dtype=jnp.int32), rows)
    return rope_tables(t, HEAD_DIM), rope_tables(row, HEAD_DIM // 2), rope_tables(col, HEAD_DIM // 2)


def fourier_mix(f):
    B, S, _ = f.shape
    fg = f.astype(jnp.float32).reshape(B, S, N_GROUPS, GROUP_W)
    y = jnp.fft.fft2(fg, axes=(1, 3), norm='ortho').real
    return y.reshape(B, S, BRANCH_W).astype(f.dtype)


def pool_mix(p, pool_w, pool_scale):
    B, S, _ = p.shape
    pf = p.astype(jnp.float32).reshape(B, S, N_GROUPS, GROUP_W)
    csum = jnp.concatenate([jnp.zeros((B, 1, N_GROUPS, GROUP_W), jnp.float32),
                            jnp.cumsum(pf, axis=1)], axis=1)
    t = jnp.arange(S)
    outs = []
    for gi, w in enumerate(POOL_WINDOWS):
        lo = jnp.maximum(t - w // 2, 0)
        hi = jnp.minimum(t + w // 2 - 1, S - 1) + 1
        cnt = (hi - lo).astype(jnp.float32)
        cg = csum[:, :, gi]
        win_sum = jnp.take(cg, hi, axis=1) - jnp.take(cg, lo, axis=1)
        outs.append(win_sum / cnt[None, :, None] - pf[:, :, gi])
    d = jnp.stack(outs, axis=2).astype(p.dtype)
    y = jnp.einsum('bsgc,gcd->bsgd', d, pool_w)
    return y.reshape(B, S, BRANCH_W) * pool_scale


def local_attention(q, k, v, sink):
    B, S, _, Dh = q.shape
    nb = S // BLOCK
    qb = q.reshape(B, nb, BLOCK, N_KV_HEADS, Q_PER_KV, Dh)
    pad = ((0, 0), (BLOCK, BLOCK), (0, 0), (0, 0))

    def band(a):
        a = jnp.pad(a, pad).reshape(B, nb + 2, BLOCK, N_KV_HEADS, Dh)
        return jnp.concatenate([a[:, :-2], a[:, 1:-1], a[:, 2:]], axis=2)

    kb, vb = band(k), band(v)
    s = jnp.einsum('bnqhgd,bnkhd->bnhgqk', qb, kb).astype(jnp.float32) * (Dh ** -0.5)
    qpos = jnp.arange(nb)[:, None] * BLOCK + jnp.arange(BLOCK)[None, :]
    kpos = jnp.arange(nb)[:, None] * BLOCK - BLOCK + jnp.arange(3 * BLOCK)[None, :]
    valid = ((jnp.abs(qpos[:, :, None] - kpos[:, None, :]) <= WINDOW)
             & (kpos[:, None, :] >= 0) & (kpos[:, None, :] < S))
    s = jnp.where(valid[None, :, None, None], s, NEG)
    sink_l = sink.astype(jnp.float32).reshape(N_KV_HEADS, Q_PER_KV)[None, None, :, :, None, None]
    sink_col = jnp.broadcast_to(sink_l, s.shape[:-1] + (1,))
    prob = jax.nn.softmax(jnp.concatenate([s, sink_col], axis=-1), axis=-1)[..., :-1]
    o = jnp.einsum('bnhgqk,bnkhd->bnqhgd', prob.astype(v.dtype), vb)
    return o.reshape(B, S, N_Q_HEADS * Dh)


def global_attention(q, k, v):
    B, S, _, Dh = q.shape
    nb = S // BLOCK
    qb = q.reshape(B, nb, BLOCK, N_KV_HEADS, Q_PER_KV, Dh).transpose(1, 0, 2, 3, 4, 5)

    def one_block(qblk):
        s = jnp.einsum('bqhgd,bkhd->bhgqk', qblk, k).astype(jnp.float32) * (Dh ** -0.5)
        prob = jax.nn.softmax(s, axis=-1)
        return jnp.einsum('bhgqk,bkhd->bqhgd', prob.astype(v.dtype), v)

    o = lax.map(one_block, qb)
    return o.transpose(1, 0, 2, 3, 4, 5).reshape(B, S, N_Q_HEADS * Dh)


def token_mixing(u, w_in, pool_w, pool_scale, attn_sink, qk_gain_q, qk_gain_k,
                 w_branch, w_gate, b_gate, w_out, rope1d, rope_row, rope_col):
    B, S, _ = u.shape
    z = u @ w_in
    f, p, ac, ad = jnp.split(z, [BRANCH_W, 2 * BRANCH_W, 2 * BRANCH_W + ATTN_IN_W], axis=-1)

    def split_qkv(a):
        q, k, v = jnp.split(a, [BRANCH_W, BRANCH_W + KV_W], axis=-1)
        return (q.reshape(B, S, N_Q_HEADS, HEAD_DIM),
                k.reshape(B, S, N_KV_HEADS, HEAD_DIM),
                v.reshape(B, S, N_KV_HEADS, HEAD_DIM))

    o_a = fourier_mix(f)
    o_b = pool_mix(p, pool_w, pool_scale)
    qc, kc, vc = split_qkv(ac)
    o_c = local_attention(apply_rope(qc, *rope1d), apply_rope(kc, *rope1d), vc, attn_sink)
    qd, kd, vd = split_qkv(ad)
    qd = apply_axial_rope(rms_norm(qd, qk_gain_q), rope_row, rope_col)
    kd = apply_axial_rope(rms_norm(kd, qk_gain_k), rope_row, rope_col)
    o_d = global_attention(qd, kd, vd)

    merged = jnp.zeros_like(u)
    for bi, o in enumerate((o_a, o_b, o_c, o_d)):
        gate = jax.nn.sigmoid(u @ w_gate[bi] + b_gate[bi])
        merged = merged + gate * (o @ w_branch[bi])
    return merged @ w_out


def trunk(x, norm_ffa, ffa_w_gate, ffa_w_up, ffa_w_down, norm_mix, w_in, pool_w, pool_scale,
          attn_sink, qk_gain_q, qk_gain_k, w_branch, w_gate, b_gate, w_out,
          norm_ffb, ffb_w_gate, ffb_w_up, ffb_w_down, final_norm):
    S = x.shape[1]
    rope1d, rope_row, rope_col = position_tables(S)
    for l in range(DEPTH):
        x = x + 0.5 * swiglu(rms_norm(x, norm_ffa[l]), ffa_w_gate[l], ffa_w_up[l], ffa_w_down[l])
        x = x + token_mixing(rms_norm(x, norm_mix[l]), w_in[l], pool_w[l], pool_scale[l],
                             attn_sink[l], qk_gain_q[l], qk_gain_k[l], w_branch[l],
                             w_gate[l], b_gate[l], w_out[l], rope1d, rope_row, rope_col)
        x = x + 0.5 * swiglu(rms_norm(x, norm_ffb[l]), ffb_w_gate[l], ffb_w_up[l], ffb_w_down[l])
    return rms_norm(x, final_norm)


def setup_inputs(seed: int = 0) -> dict:
    key = jax.random.key(seed)
    ks = jax.random.split(key, 24)
    L, D, F = DEPTH, D_MODEL, D_FF
    nrm = lambda k, shape: jax.random.normal(k, shape, jnp.float32)
    gain = lambda k, shape: 1.0 + 0.02 * nrm(k, shape)
    return {
        'x_prompt': nrm(ks[0], (BATCH, SEQ, D)),
        'x_sample': nrm(ks[1], (DEC_BATCH, DEC_SEQ, D)),
        'norm_ffa': gain(ks[2], (L, D)),
        'ffa_w_gate': nrm(ks[3], (L, D, F)) * D ** -0.5,
        'ffa_w_up': nrm(ks[4], (L, D, F)) * D ** -0.5,
        'ffa_w_down': nrm(ks[5], (L, F, D)) * F ** -0.5,
        'norm_mix': gain(ks[6], (L, D)),
        'w_in': nrm(ks[7], (L, D, IN_W)) * D ** -0.5,
        'pool_w': nrm(ks[8], (L, N_GROUPS, GROUP_W, GROUP_W)) * GROUP_W ** -0.5,
        'pool_scale': gain(ks[9], (L, BRANCH_W)),
        'attn_sink': 0.5 * nrm(ks[10], (L, N_Q_HEADS)),
        'qk_gain_q': gain(ks[11], (L, HEAD_DIM)),
        'qk_gain_k': gain(ks[12], (L, HEAD_DIM)),
        'w_branch': nrm(ks[13], (L, N_BRANCH, BRANCH_W, D)) * BRANCH_W ** -0.5,
        'w_gate': nrm(ks[14], (L, N_BRANCH, D, D)) * D ** -0.5,
        'b_gate': 0.02 * nrm(ks[15], (L, N_BRANCH, D)),
        'w_out': nrm(ks[16], (L, D, D)) * D ** -0.5,
        'norm_ffb': gain(ks[17], (L, D)),
        'ffb_w_gate': nrm(ks[18], (L, D, F)) * D ** -0.5,
        'ffb_w_up': nrm(ks[19], (L, D, F)) * D ** -0.5,
        'ffb_w_down': nrm(ks[20], (L, F, D)) * F ** -0.5,
        'final_norm': gain(ks[21], (D,)),
    }


def reference(x_prompt, x_sample, norm_ffa, ffa_w_gate, ffa_w_up, ffa_w_down, norm_mix, w_in,
              pool_w, pool_scale, attn_sink, qk_gain_q, qk_gain_k, w_branch, w_gate, b_gate,
              w_out, norm_ffb, ffb_w_gate, ffb_w_up, ffb_w_down, final_norm):
    y_prompt = trunk(x_prompt, norm_ffa, ffa_w_gate, ffa_w_up, ffa_w_down, norm_mix, w_in,
                     pool_w, pool_scale, attn_sink, qk_gain_q, qk_gain_k, w_branch, w_gate,
                     b_gate, w_out, norm_ffb, ffb_w_gate, ffb_w_up, ffb_w_down, final_norm)
    y_sample = trunk(x_sample, norm_ffa, ffa_w_gate, ffa_w_up, ffa_w_down, norm_mix, w_in,
                     pool_w, pool_scale, attn_sink, qk_gain_q, qk_gain_k, w_branch, w_gate,
                     b_gate, w_out, norm_ffb, ffb_w_gate, ffb_w_up, ffb_w_down, final_norm)
    return (y_prompt, y_sample)
```

```python
import functools
import math

import jax
import jax.numpy as jnp
from jax import lax
from jax.experimental import pallas as pl
from jax.experimental.pallas import tpu as pltpu

BF16 = jnp.bfloat16
F32 = jnp.float32

N_BRANCH = 4
BRANCH_W = 512
N_GROUPS = 4
GROUP_W = 128
POOL_WINDOWS = (2, 4, 8, 16)
HEAD_DIM = 128
N_Q_HEADS = 4
N_KV_HEADS = 2
Q_PER_KV = 2
KV_W = N_KV_HEADS * HEAD_DIM
ATTN_IN_W = BRANCH_W + 2 * KV_W
GRID_W = 64
WINDOW = 128
ROPE_THETA = 10000.0
EPS = 1e-6
NEG = -1e30
ATTN_SCALE = HEAD_DIM ** -0.5

VMEM_LIMIT_BYTES = 48 * 1024 * 1024
LANE = 128
SUBLANE = 8


def _tile(dim, pref, align):
    t = min(pref, dim)
    t -= t % align
    while t >= align:
        if dim % t == 0:
            return t
        t -= align
    return dim


def _params(*sem):
    return pltpu.CompilerParams(dimension_semantics=sem, vmem_limit_bytes=VMEM_LIMIT_BYTES)


def _rmsnorm_kernel(x_ref, g_ref, o_ref):
    x = x_ref[...]
    ms = jnp.mean(x * x, axis=-1, keepdims=True)
    o_ref[...] = (x * lax.rsqrt(ms + EPS) * g_ref[...]).astype(o_ref.dtype)


def rmsnorm(x, g, out_dtype):
    M, D = x.shape
    tm = _tile(M, 512, SUBLANE)
    return pl.pallas_call(
        _rmsnorm_kernel,
        grid=(M // tm,),
        in_specs=[pl.BlockSpec((tm, D), lambda i: (i, 0)),
                  pl.BlockSpec((1, D), lambda i: (0, 0))],
        out_specs=pl.BlockSpec((tm, D), lambda i: (i, 0)),
        out_shape=jax.ShapeDtypeStruct((M, D), out_dtype),
        compiler_params=_params("parallel"),
        name="rmsnorm",
    )(x, g.reshape(1, D).astype(F32))


def _matmul_kernel(*refs, nk, has_res, alpha):
    if has_res:
        x_ref, w_ref, r_ref, o_ref = refs[:4]
        scratch = refs[4:]
    else:
        x_ref, w_ref, o_ref = refs[:3]
        r_ref = None
        scratch = refs[3:]
    part = jnp.dot(x_ref[...].astype(BF16), w_ref[...], preferred_element_type=F32)

    def finish(acc):
        if has_res:
            acc = r_ref[...] + alpha * acc
        o_ref[...] = acc.astype(o_ref.dtype)

    if nk == 1:
        finish(part)
    else:
        acc_ref = scratch[0]
        k = pl.program_id(2)

        @pl.when(k == 0)
        def _():
            acc_ref[...] = part

        @pl.when(jnp.logical_and(k > 0, k < nk - 1))
        def _():
            acc_ref[...] += part

        @pl.when(k == nk - 1)
        def _():
            finish(acc_ref[...] + part)


def matmul(x, w, *, out_dtype, res=None, alpha=1.0, tm=1024, tn=512, tk=None):
    M = x.shape[0]
    K, N = w.shape
    tm = _tile(M, tm, SUBLANE)
    tn = _tile(N, tn, LANE)
    tk = K if tk is None else _tile(K, tk, LANE)
    nk = K // tk
    in_specs = [pl.BlockSpec((tm, tk), lambda i, j, k: (i, k)),
                pl.BlockSpec((tk, tn), lambda i, j, k: (k, j))]
    args = [x, w]
    if res is not None:
        in_specs.append(pl.BlockSpec((tm, tn), lambda i, j, k: (i, j)))
        args.append(res)
    scratch = [pltpu.VMEM((tm, tn), F32)] if nk > 1 else []
    return pl.pallas_call(
        functools.partial(_matmul_kernel, nk=nk, has_res=res is not None, alpha=alpha),
        grid=(M // tm, N // tn, nk),
        in_specs=in_specs,
        out_specs=pl.BlockSpec((tm, tn), lambda i, j, k: (i, j)),
        out_shape=jax.ShapeDtypeStruct((M, N), out_dtype),
        scratch_shapes=scratch,
        compiler_params=_params("parallel", "parallel", "arbitrary"),
        name="matmul",
    )(*args)


def _ffn_up_kernel(x_ref, wg_ref, wu_ref, o_ref):
    x = x_ref[...]
    g = jnp.dot(x, wg_ref[...], preferred_element_type=F32)
    u = jnp.dot(x, wu_ref[...], preferred_element_type=F32)
    o_ref[...] = (g * jax.nn.sigmoid(g) * u).astype(o_ref.dtype)


def ffn_up(xn, wg, wu, *, tm=1024, tn=512):
    M, K = xn.shape
    N = wg.shape[1]
    tm = _tile(M, tm, SUBLANE)
    tn = _tile(N, tn, LANE)
    return pl.pallas_call(
        _ffn_up_kernel,
        grid=(M // tm, N // tn),
        in_specs=[pl.BlockSpec((tm, K), lambda i, j: (i, 0)),
                  pl.BlockSpec((K, tn), lambda i, j: (0, j)),
                  pl.BlockSpec((K, tn), lambda i, j: (0, j))],
        out_specs=pl.BlockSpec((tm, tn), lambda i, j: (i, j)),
        out_shape=jax.ShapeDtypeStruct((M, N), BF16),
        compiler_params=_params("parallel", "parallel"),
        name="ffn_up",
    )(xn, wg, wu)


def _qkv_kernel(ac_ref, ad_ref, c1_ref, s1_ref, ca_ref, saa_ref, sab_ref, gq_ref, gk_ref,
                qc_ref, kc_ref, vc_ref, qd_ref, kd_ref, vd_ref):
    c1, s1 = c1_ref[...], s1_ref[...]
    ca, saa, sab = ca_ref[...], saa_ref[...], sab_ref[...]

    def rope1d(x):
        return x * c1 + pltpu.roll(x, HEAD_DIM // 2, axis=1) * s1

    def rope_axial(x):
        return (x * ca + pltpu.roll(x, HEAD_DIM - HEAD_DIM // 4, axis=1) * saa
                + pltpu.roll(x, HEAD_DIM // 4, axis=1) * sab)

    def norm(x, g):
        ms = jnp.mean(x * x, axis=-1, keepdims=True)
        return x * lax.rsqrt(ms + EPS) * g

    def head(ref, off, h):
        return ref[0, :, off + h * HEAD_DIM: off + (h + 1) * HEAD_DIM]

    gq, gk = gq_ref[...], gk_ref[...]
    for h in range(N_Q_HEADS):
        qc_ref[0, h] = rope1d(head(ac_ref, 0, h)).astype(BF16)
        qd_ref[0, h] = rope_axial(norm(head(ad_ref, 0, h), gq)).astype(BF16)
    for h in range(N_KV_HEADS):
        kc_ref[0, h] = rope1d(head(ac_ref, BRANCH_W, h)).astype(BF16)
        vc_ref[0, h] = head(ac_ref, BRANCH_W + KV_W, h).astype(BF16)
        kd_ref[0, h] = rope_axial(norm(head(ad_ref, BRANCH_W, h), gk)).astype(BF16)
        vd_ref[0, h] = head(ad_ref, BRANCH_W + KV_W, h).astype(BF16)


def qkv_prep(z, tables, gq, gk):
    B, S, _ = z.shape
    T = _tile(S, 256, 16)
    tab = pl.BlockSpec((T, HEAD_DIM), lambda b, i: (i, 0))
    gain = pl.BlockSpec((1, HEAD_DIM), lambda b, i: (0, 0))

    def out(h):
        return pl.BlockSpec((1, h, T, HEAD_DIM), lambda b, i: (b, 0, i, 0))

    def shape(h):
        return jax.ShapeDtypeStruct((B, h, S, HEAD_DIM), BF16)

    hq, hk = N_Q_HEADS, N_KV_HEADS
    return pl.pallas_call(
        _qkv_kernel,
        grid=(B, S // T),
        in_specs=[pl.BlockSpec((1, T, ATTN_IN_W), lambda b, i: (b, i, 1)),
                  pl.BlockSpec((1, T, ATTN_IN_W), lambda b, i: (b, i, 2)),
                  tab, tab, tab, tab, tab, gain, gain],
        out_specs=[out(hq), out(hk), out(hk), out(hq), out(hk), out(hk)],
        out_shape=[shape(hq), shape(hk), shape(hk), shape(hq), shape(hk), shape(hk)],
        compiler_params=_params("parallel", "parallel"),
        name="qkv_prep",
    )(z, z, tables["cos1"], tables["sin1"], tables["cos_ax"], tables["sin_ax_a"], tables["sin_ax_b"],
      gq.reshape(1, HEAD_DIM).astype(F32), gk.reshape(1, HEAD_DIM).astype(F32))


def _local_attn_kernel(q_ref, kp_ref, kc_ref, kn_ref, vp_ref, vc_ref, vn_ref, sink_ref, o_ref, *, tq, seq):
    i = pl.program_id(2)
    q = q_ref[0].reshape(Q_PER_KV * tq, HEAD_DIM)
    k = jnp.concatenate([kp_ref[0, 0], kc_ref[0, 0], kn_ref[0, 0]], axis=0)
    v = jnp.concatenate([vp_ref[0, 0], vc_ref[0, 0], vn_ref[0, 0]], axis=0)
    s = lax.dot_general(q, k, (((1,), (1,)), ((), ())), preferred_element_type=F32) * ATTN_SCALE
    row = lax.broadcasted_iota(jnp.int32, s.shape, 0)
    col = lax.broadcasted_iota(jnp.int32, s.shape, 1)
    qpos = i * tq + jnp.where(row >= tq, row - tq, row)
    kpos = i * tq - WINDOW + col
    valid = (jnp.abs(qpos - kpos) <= WINDOW) & (kpos >= 0) & (kpos < seq)
    s = jnp.where(valid, s, NEG)
    sink = sink_ref[0]
    m = jnp.maximum(jnp.max(s, axis=-1, keepdims=True), sink)
    p = jnp.exp(s - m)
    denom = jnp.sum(p, axis=-1, keepdims=True) + jnp.exp(sink - m)
    o = jnp.dot((p / denom).astype(BF16), v, preferred_element_type=F32)
    for g in range(Q_PER_KV):
        o_ref[0, :, g * HEAD_DIM:(g + 1) * HEAD_DIM] = o[g * tq:(g + 1) * tq].astype(o_ref.dtype)


def local_attention(q, k, v, sink):
    B, _, S, _ = q.shape
    tq = _tile(S, 256, WINDOW)
    r = tq // WINDOW
    nb = S // WINDOW
    qspec = pl.BlockSpec((1, Q_PER_KV, tq, HEAD_DIM), lambda b, h, i: (b, h, i, 0))
    prev = pl.BlockSpec((1, 1, WINDOW, HEAD_DIM), lambda b, h, i: (b, h, jnp.maximum(i * r - 1, 0), 0))
    cur = pl.BlockSpec((1, 1, tq, HEAD_DIM), lambda b, h, i: (b, h, i, 0))
    nxt = pl.BlockSpec((1, 1, WINDOW, HEAD_DIM), lambda b, h, i: (b, h, jnp.minimum((i + 1) * r, nb - 1), 0))
    sink_rows = jnp.repeat(sink.astype(F32).reshape(N_KV_HEADS, Q_PER_KV), tq, axis=1)[:, :, None]
    return pl.pallas_call(
        functools.partial(_local_attn_kernel, tq=tq, seq=S),
        grid=(B, N_KV_HEADS, S // tq),
        in_specs=[qspec, prev, cur, nxt, prev, cur, nxt,
                  pl.BlockSpec((1, Q_PER_KV * tq, 1), lambda b, h, i: (h, 0, 0))],
        out_specs=pl.BlockSpec((1, tq, Q_PER_KV * HEAD_DIM), lambda b, h, i: (b, i, h)),
        out_shape=jax.ShapeDtypeStruct((B, S, BRANCH_W), BF16),
        compiler_params=_params("parallel", "parallel", "parallel"),
        name="local_attention",
    )(q, k, k, k, v, v, v, sink_rows)


def _global_attn_kernel(q_ref, k_ref, v_ref, o_ref, m_ref, l_ref, acc_ref, *, tq, nk):
    j = pl.program_id(3)

    @pl.when(j == 0)
    def _():
        m_ref[...] = jnp.full(m_ref.shape, NEG, F32)
        l_ref[...] = jnp.zeros(l_ref.shape, F32)
        acc_ref[...] = jnp.zeros(acc_ref.shape, F32)

    q = q_ref[0].reshape(Q_PER_KV * tq, HEAD_DIM)
    s = lax.dot_general(q, k_ref[0, 0], (((1,), (1,)), ((), ())), preferred_element_type=F32) * ATTN_SCALE
    m_prev = m_ref[...]
    m_new = jnp.maximum(m_prev, jnp.max(s, axis=-1, keepdims=True))
    a = jnp.exp(m_prev - m_new)
    p = jnp.exp(s - m_new)
    l_ref[...] = a * l_ref[...] + jnp.sum(p, axis=-1, keepdims=True)
    acc_ref[...] = a * acc_ref[...] + jnp.dot(p.astype(BF16), v_ref[0, 0], preferred_element_type=F32)
    m_ref[...] = m_new

    @pl.when(j == nk - 1)
    def _():
        o = acc_ref[...] / l_ref[...]
        for g in range(Q_PER_KV):
            o_ref[0, :, g * HEAD_DIM:(g + 1) * HEAD_DIM] = o[g * tq:(g + 1) * tq].astype(o_ref.dtype)


def global_attention(q, k, v):
    B, _, S, _ = q.shape
    tq = _tile(S, 256, 16)
    tk = _tile(S, 2048, LANE)
    nk = S // tk
    rows = Q_PER_KV * tq
    return pl.pallas_call(
        functools.partial(_global_attn_kernel, tq=tq, nk=nk),
        grid=(B, N_KV_HEADS, S // tq, nk),
        in_specs=[pl.BlockSpec((1, Q_PER_KV, tq, HEAD_DIM), lambda b, h, i, j: (b, h, i, 0)),
                  pl.BlockSpec((1, 1, tk, HEAD_DIM), lambda b, h, i, j: (b, h, j, 0)),
                  pl.BlockSpec((1, 1, tk, HEAD_DIM), lambda b, h, i, j: (b, h, j, 0))],
        out_specs=pl.BlockSpec((1, tq, Q_PER_KV * HEAD_DIM), lambda b, h, i, j: (b, i, h)),
        out_shape=jax.ShapeDtypeStruct((B, S, BRANCH_W), BF16),
        scratch_shapes=[pltpu.VMEM((rows, 1), F32), pltpu.VMEM((rows, 1), F32),
                        pltpu.VMEM((rows, HEAD_DIM), F32)],
        compiler_params=_params("parallel", "parallel", "parallel", "arbitrary"),
        name="global_attention",
    )(q, k, v)


def _dft_kernel(c_ref, s_ref, ab_ref, o_ref, acc_ref, *, nk):
    k = pl.program_id(2)
    part = (jnp.dot(c_ref[...], ab_ref[0, :, :BRANCH_W], preferred_element_type=F32)
            + jnp.dot(s_ref[...], ab_ref[0, :, BRANCH_W:], preferred_element_type=F32))

    @pl.when(k == 0)
    def _():
        acc_ref[...] = part

    @pl.when(k > 0)
    def _():
        acc_ref[...] += part

    @pl.when(k == nk - 1)
    def _():
        o_ref[0] = acc_ref[...].astype(o_ref.dtype)


def dft_positions(cos_s, neg_sin_s, ab):
    B, S, _ = ab.shape
    tm = _tile(S, 2048, 16)
    tk = _tile(S, 1024, LANE)
    nk = S // tk
    return pl.pallas_call(
        functools.partial(_dft_kernel, nk=nk),
        grid=(B, S // tm, nk),
        in_specs=[pl.BlockSpec((tm, tk), lambda b, i, k: (i, k)),
                  pl.BlockSpec((tm, tk), lambda b, i, k: (i, k)),
                  pl.BlockSpec((1, tk, 2 * BRANCH_W), lambda b, i, k: (b, k, 0))],
        out_specs=pl.BlockSpec((1, tm, BRANCH_W), lambda b, i, k: (b, i, 0)),
        out_shape=jax.ShapeDtypeStruct((B, S, BRANCH_W), BF16),
        scratch_shapes=[pltpu.VMEM((tm, BRANCH_W), F32)],
        compiler_params=_params("parallel", "parallel", "arbitrary"),
        name="dft_positions",
    )(cos_s, neg_sin_s, ab)


def _pool_kernel(pp_ref, pc_ref, pn_ref, band_ref, pw_ref, ps_ref, o_ref):
    def split(x):
        hi = x.astype(BF16)
        return hi, (x - hi.astype(F32)).astype(BF16)

    for g in range(N_GROUPS):
        sl = slice(g * GROUP_W, (g + 1) * GROUP_W)
        pc = pc_ref[0, :, sl]
        mean = jnp.zeros(pc.shape, F32)
        for j, ref in enumerate((pp_ref, pc_ref, pn_ref)):
            hi, lo = split(ref[0, :, sl])
            band = band_ref[g, 0, j]
            mean = mean + jnp.dot(band, hi, preferred_element_type=F32)
            mean = mean + jnp.dot(band, lo, preferred_element_type=F32)
        d = (mean - pc).astype(BF16)
        y = jnp.dot(d, pw_ref[g], preferred_element_type=F32) * ps_ref[:, sl]
        o_ref[0, :, sl] = y.astype(o_ref.dtype)


def pool_mix(z, band, pool_w, pool_scale):
    B, S, _ = z.shape
    T = band.shape[-1]
    nb = S // T
    return pl.pallas_call(
        _pool_kernel,
        grid=(B, nb),
        in_specs=[pl.BlockSpec((1, T, BRANCH_W), lambda b, i: (b, jnp.maximum(i - 1, 0), 1)),
                  pl.BlockSpec((1, T, BRANCH_W), lambda b, i: (b, i, 1)),
                  pl.BlockSpec((1, T, BRANCH_W), lambda b, i: (b, jnp.minimum(i + 1, nb - 1), 1)),
                  pl.BlockSpec((N_GROUPS, 1, 3, T, T), lambda b, i: (0, i, 0, 0, 0)),
                  pl.BlockSpec((N_GROUPS, GROUP_W, GROUP_W), lambda b, i: (0, 0, 0)),
                  pl.BlockSpec((1, BRANCH_W), lambda b, i: (0, 0))],
        out_specs=pl.BlockSpec((1, T, BRANCH_W), lambda b, i: (b, i, 0)),
        out_shape=jax.ShapeDtypeStruct((B, S, BRANCH_W), BF16),
        compiler_params=_params("parallel", "parallel"),
        name="pool_mix",
    )(z, z, z, band, pool_w, pool_scale.reshape(1, BRANCH_W).astype(F32))


def _merge_kernel(u_ref, wg_ref, bg_ref, o_ref, wb_ref, out_ref, acc_ref):
    bi = pl.program_id(2)
    gate = jax.nn.sigmoid(jnp.dot(u_ref[...], wg_ref[0], preferred_element_type=F32) + bg_ref[0])
    term = gate * jnp.dot(o_ref[0], wb_ref[0], preferred_element_type=F32)

    @pl.when(bi == 0)
    def _():
        acc_ref[...] = term

    @pl.when(jnp.logical_and(bi > 0, bi < N_BRANCH - 1))
    def _():
        acc_ref[...] += term

    @pl.when(bi == N_BRANCH - 1)
    def _():
        out_ref[...] = (acc_ref[...] + term).astype(out_ref.dtype)


def merge_branches(u, w_gate, b_gate, o_stack, w_branch, *, tm=1024, tn=512):
    M, D = u.shape
    tm = _tile(M, tm, SUBLANE)
    tn = _tile(D, tn, LANE)
    return pl.pallas_call(
        _merge_kernel,
        grid=(M // tm, D // tn, N_BRANCH),
        in_specs=[pl.BlockSpec((tm, D), lambda i, j, bi: (i, 0)),
                  pl.BlockSpec((1, D, tn), lambda i, j, bi: (bi, 0, j)),
                  pl.BlockSpec((1, 1, tn), lambda i, j, bi: (bi, 0, j)),
                  pl.BlockSpec((1, tm, BRANCH_W), lambda i, j, bi: (bi, i, 0)),
                  pl.BlockSpec((1, BRANCH_W, tn), lambda i, j, bi: (bi, 0, j))],
        out_specs=pl.BlockSpec((tm, tn), lambda i, j, bi: (i, j)),
        out_shape=jax.ShapeDtypeStruct((M, D), BF16),
        scratch_shapes=[pltpu.VMEM((tm, tn), F32)],
        compiler_params=_params("parallel", "parallel", "arbitrary"),
        name="merge_branches",
    )(u, w_gate, b_gate.reshape(N_BRANCH, 1, D).astype(F32), o_stack, w_branch)


def _rope(pos, dim):
    inv = ROPE_THETA ** (-jnp.arange(0, dim, 2, dtype=F32) / dim)
    ang = pos.astype(F32)[:, None] * inv[None, :]
    ang = jnp.concatenate([ang, ang], axis=-1)
    return jnp.cos(ang), jnp.sin(ang)


def make_tables(S):
    t = jnp.arange(S, dtype=jnp.int32)
    lane = jnp.arange(HEAD_DIM)
    half, quarter = HEAD_DIM // 2, HEAD_DIM // 4

    cos1, sin1 = _rope(t, HEAD_DIM)
    sin1 = sin1 * jnp.where(lane < half, -1.0, 1.0)[None, :]
    cr, sr = _rope(t // GRID_W, half)
    cc, sc = _rope(t % GRID_W, half)
    cos_ax = jnp.concatenate([cr, cc], axis=-1)
    sin_ax = jnp.concatenate([sr, sc], axis=-1)
    first = (lane % half) < quarter
    sin_ax_a = jnp.where(first[None, :], -sin_ax, 0.0)
    sin_ax_b = jnp.where(first[None, :], 0.0, sin_ax)

    jk = (t[:, None] * t[None, :]) % S
    ang = jk.astype(F32) * (2.0 * math.pi / S)
    cos_s = jnp.cos(ang).astype(BF16)
    neg_sin_s = (-jnp.sin(ang)).astype(BF16)

    c = jnp.arange(GROUP_W, dtype=jnp.int32)
    angc = ((c[:, None] * c[None, :]) % GROUP_W).astype(F32) * (2.0 * math.pi / GROUP_W)
    scale = 1.0 / math.sqrt(S * GROUP_W)
    eye = jnp.eye(N_GROUPS, dtype=F32)
    w_chan = jnp.concatenate([jnp.kron(eye, jnp.cos(angc) * scale),
                              jnp.kron(eye, jnp.sin(angc) * scale)], axis=1).astype(BF16)

    T = _tile(S, 256, LANE)
    nb = S // T
    rows = t.reshape(nb, 1, T, 1)
    cols = ((jnp.arange(nb)[:, None] - 1 + jnp.arange(3)[None, :]) * T)[:, :, None, None] \
        + jnp.arange(T)[None, None, None, :]
    bands = []
    for w in POOL_WINDOWS:
        lo = jnp.maximum(rows - w // 2, 0)
        hi = jnp.minimum(rows + w // 2 - 1, S - 1) + 1
        inside = (cols >= lo) & (cols < hi)
        bands.append(jnp.where(inside, 1.0 / (hi - lo).astype(F32), 0.0))
    band = jnp.stack(bands).astype(BF16)

    return dict(cos1=cos1, sin1=sin1, cos_ax=cos_ax, sin_ax_a=sin_ax_a, sin_ax_b=sin_ax_b,
                cos_s=cos_s, neg_sin_s=neg_sin_s, w_chan=w_chan, band=band)


def trunk(x, p):
    B, S, D = x.shape
    M = B * S
    depth = p["norm_ffa"].shape[0]
    tab = make_tables(S)
    x = x.reshape(M, D)

    def ffn(x, g, wg, wu, wd):
        h = ffn_up(rmsnorm(x, g, BF16), wg, wu)
        return matmul(h, wd, out_dtype=F32, res=x, alpha=0.5, tk=wd.shape[0] // 2)

    for l in range(depth):
        x = ffn(x, p["norm_ffa"][l], p["ffa_w_gate"][l], p["ffa_w_up"][l], p["ffa_w_down"][l])

        u = rmsnorm(x, p["norm_mix"][l], BF16)
        z = matmul(u, p["w_in"][l], out_dtype=F32)
        z3 = z.reshape(B, S, -1)
        ab = matmul(z, tab["w_chan"], out_dtype=BF16)
        o_a = dft_positions(tab["cos_s"], tab["neg_sin_s"], ab.reshape(B, S, -1))
        o_b = pool_mix(z3, tab["band"], p["pool_w"][l], p["pool_scale"][l])
        qc, kc, vc, qd, kd, vd = qkv_prep(z3, tab, p["qk_gain_q"][l], p["qk_gain_k"][l])
        o_c = local_attention(qc, kc, vc, p["attn_sink"][l])
        o_d = global_attention(qd, kd, vd)
        o_stack = jnp.stack([o_a, o_b, o_c, o_d]).reshape(N_BRANCH, M, BRANCH_W)
        merged = merge_branches(u, p["w_gate"][l], p["b_gate"][l], o_stack, p["w_branch"][l])
        x = matmul(merged, p["w_out"][l], out_dtype=F32, res=x, alpha=1.0)

        x = ffn(x, p["norm_ffb"][l], p["ffb_w_gate"][l], p["ffb_w_up"][l], p["ffb_w_down"][l])

    return rmsnorm(x, p["final_norm"], F32).reshape(B, S, D)


_MATMUL_WEIGHTS = ("ffa_w_gate", "ffa_w_up", "ffa_w_down", "w_in", "pool_w", "w_branch", "w_gate",
                   "w_out", "ffb_w_gate", "ffb_w_up", "ffb_w_down")


@jax.jit
def _forward(x_prompt, x_sample, p):
    p = {k: (v.astype(BF16) if k in _MATMUL_WEIGHTS else v) for k, v in p.items()}
    return trunk(x_prompt, p), trunk(x_sample, p)


def kernel(x_prompt, x_sample, norm_ffa, ffa_w_gate, ffa_w_up, ffa_w_down, norm_mix, w_in, pool_w, pool_scale,
           attn_sink, qk_gain_q, qk_gain_k, w_branch, w_gate, b_gate, w_out, norm_ffb, ffb_w_gate, ffb_w_up,
           ffb_w_down, final_norm):
    p = dict(norm_ffa=norm_ffa, ffa_w_gate=ffa_w_gate, ffa_w_up=ffa_w_up, ffa_w_down=ffa_w_down,
             norm_mix=norm_mix, w_in=w_in, pool_w=pool_w, pool_scale=pool_scale, attn_sink=attn_sink,
             qk_gain_q=qk_gain_q, qk_gain_k=qk_gain_k, w_branch=w_branch, w_gate=w_gate, b_gate=b_gate,
             w_out=w_out, norm_ffb=norm_ffb, ffb_w_gate=ffb_w_gate, ffb_w_up=ffb_w_up,
             ffb_w_down=ffb_w_down, final_norm=final_norm)
    return _forward(x_prompt, x_sample, p)
```

```python
import functools
import math

import jax
import jax.numpy as jnp
from jax import lax
from jax.experimental import pallas as pl
from jax.experimental.pallas import tpu as pltpu

BF16 = jnp.bfloat16
F32 = jnp.float32

N_BRANCH = 4
BRANCH_W = 512
N_GROUPS = 4
GROUP_W = 128
POOL_WINDOWS = (2, 4, 8, 16)
HEAD_DIM = 128
N_Q_HEADS = 4
N_KV_HEADS = 2
Q_PER_KV = 2
KV_W = N_KV_HEADS * HEAD_DIM
ATTN_IN_W = BRANCH_W + 2 * KV_W
GRID_W = 64
WINDOW = 128
ROPE_THETA = 10000.0
EPS = 1e-6
NEG = -1e30
ATTN_SCALE = HEAD_DIM ** -0.5

VMEM_LIMIT_BYTES = 48 * 1024 * 1024
LANE = 128
SUBLANE = 8


def _tile(dim, pref, align):
    t = min(pref, dim)
    t -= t % align
    while t >= align:
        if dim % t == 0:
            return t
        t -= align
    return dim


def _params(*sem):
    return pltpu.CompilerParams(dimension_semantics=sem, vmem_limit_bytes=VMEM_LIMIT_BYTES)


def _rmsnorm_kernel(x_ref, g_ref, o_ref):
    x = x_ref[...]
    ms = jnp.mean(x * x, axis=-1, keepdims=True)
    o_ref[...] = (x * lax.rsqrt(ms + EPS) * g_ref[...]).astype(o_ref.dtype)


def rmsnorm(x, g, out_dtype):
    M, D = x.shape
    tm = _tile(M, 512, SUBLANE)
    return pl.pallas_call(
        _rmsnorm_kernel,
        grid=(M // tm,),
        in_specs=[pl.BlockSpec((tm, D), lambda i: (i, 0)),
                  pl.BlockSpec((1, D), lambda i: (0, 0))],
        out_specs=pl.BlockSpec((tm, D), lambda i: (i, 0)),
        out_shape=jax.ShapeDtypeStruct((M, D), out_dtype),
        compiler_params=_params("parallel"),
        name="rmsnorm",
    )(x, g.reshape(1, D).astype(F32))


def _matmul_kernel(*refs, nk, has_res, alpha):
    if has_res:
        x_ref, w_ref, r_ref, o_ref = refs[:4]
        scratch = refs[4:]
    else:
        x_ref, w_ref, o_ref = refs[:3]
        r_ref = None
        scratch = refs[3:]
    part = jnp.dot(x_ref[...].astype(BF16), w_ref[...], preferred_element_type=F32)

    def finish(acc):
        if has_res:
            acc = r_ref[...] + alpha * acc
        o_ref[...] = acc.astype(o_ref.dtype)

    if nk == 1:
        finish(part)
    else:
        acc_ref = scratch[0]
        k = pl.program_id(2)

        @pl.when(k == 0)
        def _():
            acc_ref[...] = part

        @pl.when(jnp.logical_and(k > 0, k < nk - 1))
        def _():
            acc_ref[...] += part

        @pl.when(k == nk - 1)
        def _():
            finish(acc_ref[...] + part)


def matmul(x, w, *, out_dtype, res=None, alpha=1.0, tm=1024, tn=512, tk=None):
    M = x.shape[0]
    K, N = w.shape
    tm = _tile(M, tm, SUBLANE)
    tn = _tile(N, tn, LANE)
    tk = K if tk is None else _tile(K, tk, LANE)
    nk = K // tk
    in_specs = [pl.BlockSpec((tm, tk), lambda i, j, k: (i, k)),
                pl.BlockSpec((tk, tn), lambda i, j, k: (k, j))]
    args = [x, w]
    if res is not None:
        in_specs.append(pl.BlockSpec((tm, tn), lambda i, j, k: (i, j)))
        args.append(res)
    scratch = [pltpu.VMEM((tm, tn), F32)] if nk > 1 else []
    return pl.pallas_call(
        functools.partial(_matmul_kernel, nk=nk, has_res=res is not None, alpha=alpha),
        grid=(M // tm, N // tn, nk),
        in_specs=in_specs,
        out_specs=pl.BlockSpec((tm, tn), lambda i, j, k: (i, j)),
        out_shape=jax.ShapeDtypeStruct((M, N), out_dtype),
        scratch_shapes=scratch,
        compiler_params=_params("parallel", "parallel", "arbitrary"),
        name="matmul",
    )(*args)


def _ffn_up_kernel(x_ref, wg_ref, wu_ref, o_ref):
    x = x_ref[...]
    g = jnp.dot(x, wg_ref[...], preferred_element_type=F32)
    u = jnp.dot(x, wu_ref[...], preferred_element_type=F32)
    o_ref[...] = (g * jax.nn.sigmoid(g) * u).astype(o_ref.dtype)


def ffn_up(xn, wg, wu, *, tm=1024, tn=512):
    M, K = xn.shape
    N = wg.shape[1]
    tm = _tile(M, tm, SUBLANE)
    tn = _tile(N, tn, LANE)
    return pl.pallas_call(
        _ffn_up_kernel,
        grid=(M // tm, N // tn),
        in_specs=[pl.BlockSpec((tm, K), lambda i, j: (i, 0)),
                  pl.BlockSpec((K, tn), lambda i, j: (0, j)),
                  pl.BlockSpec((K, tn), lambda i, j: (0, j))],
        out_specs=pl.BlockSpec((tm, tn), lambda i, j: (i, j)),
        out_shape=jax.ShapeDtypeStruct((M, N), BF16),
        compiler_params=_params("parallel", "parallel"),
        name="ffn_up",
    )(xn, wg, wu)


def _qkv_kernel(ac_ref, ad_ref, c1_ref, s1_ref, ca_ref, saa_ref, sab_ref, gq_ref, gk_ref,
                qc_ref, kc_ref, vc_ref, qd_ref, kd_ref, vd_ref):
    c1, s1 = c1_ref[...], s1_ref[...]
    ca, saa, sab = ca_ref[...], saa_ref[...], sab_ref[...]

    def rope1d(x):
        return x * c1 + pltpu.roll(x, HEAD_DIM // 2, axis=1) * s1

    def rope_axial(x):
        return (x * ca + pltpu.roll(x, HEAD_DIM - HEAD_DIM // 4, axis=1) * saa
                + pltpu.roll(x, HEAD_DIM // 4, axis=1) * sab)

    def norm(x, g):
        ms = jnp.mean(x * x, axis=-1, keepdims=True)
        return x * lax.rsqrt(ms + EPS) * g

    def head(ref, off, h):
        return ref[0, :, off + h * HEAD_DIM: off + (h + 1) * HEAD_DIM]

    gq, gk = gq_ref[...], gk_ref[...]
    for h in range(N_Q_HEADS):
        qc_ref[0, h] = rope1d(head(ac_ref, 0, h)).astype(BF16)
        qd_ref[0, h] = rope_axial(norm(head(ad_ref, 0, h), gq)).astype(BF16)
    for h in range(N_KV_HEADS):
        kc_ref[0, h] = rope1d(head(ac_ref, BRANCH_W, h)).astype(BF16)
        vc_ref[0, h] = head(ac_ref, BRANCH_W + KV_W, h).astype(BF16)
        kd_ref[0, h] = rope_axial(norm(head(ad_ref, BRANCH_W, h), gk)).astype(BF16)
        vd_ref[0, h] = head(ad_ref, BRANCH_W + KV_W, h).astype(BF16)


def qkv_prep(z, tables, gq, gk):
    B, S, _ = z.shape
    T = _tile(S, 256, 16)
    tab = pl.BlockSpec((T, HEAD_DIM), lambda b, i: (i, 0))
    gain = pl.BlockSpec((1, HEAD_DIM), lambda b, i: (0, 0))

    def out(h):
        return pl.BlockSpec((1, h, T, HEAD_DIM), lambda b, i: (b, 0, i, 0))

    def shape(h):
        return jax.ShapeDtypeStruct((B, h, S, HEAD_DIM), BF16)

    hq, hk = N_Q_HEADS, N_KV_HEADS
    return pl.pallas_call(
        _qkv_kernel,
        grid=(B, S // T),
        in_specs=[pl.BlockSpec((1, T, ATTN_IN_W), lambda b, i: (b, i, 1)),
                  pl.BlockSpec((1, T, ATTN_IN_W), lambda b, i: (b, i, 2)),
                  tab, tab, tab, tab, tab, gain, gain],
        out_specs=[out(hq), out(hk), out(hk), out(hq), out(hk), out(hk)],
        out_shape=[shape(hq), shape(hk), shape(hk), shape(hq), shape(hk), shape(hk)],
        compiler_params=_params("parallel", "parallel"),
        name="qkv_prep",
    )(z, z, tables["cos1"], tables["sin1"], tables["cos_ax"], tables["sin_ax_a"], tables["sin_ax_b"],
      gq.reshape(1, HEAD_DIM).astype(F32), gk.reshape(1, HEAD_DIM).astype(F32))


def _local_attn_kernel(q_ref, kp_ref, kc_ref, kn_ref, vp_ref, vc_ref, vn_ref, sink_ref, o_ref, *, tq, seq):
    i = pl.program_id(2)
    q = q_ref[0].reshape(Q_PER_KV * tq, HEAD_DIM)
    k = jnp.concatenate([kp_ref[0, 0], kc_ref[0, 0], kn_ref[0, 0]], axis=0)
    v = jnp.concatenate([vp_ref[0, 0], vc_ref[0, 0], vn_ref[0, 0]], axis=0)
    s = lax.dot_general(q, k, (((1,), (1,)), ((), ())), preferred_element_type=F32) * ATTN_SCALE
    row = lax.broadcasted_iota(jnp.int32, s.shape, 0)
    col = lax.broadcasted_iota(jnp.int32, s.shape, 1)
    qpos = i * tq + jnp.where(row >= tq, row - tq, row)
    kpos = i * tq - WINDOW + col
    valid = (jnp.abs(qpos - kpos) <= WINDOW) & (kpos >= 0) & (kpos < seq)
    s = jnp.where(valid, s, NEG)
    sink = sink_ref[0]
    m = jnp.maximum(jnp.max(s, axis=-1, keepdims=True), sink)
    p = jnp.exp(s - m)
    denom = jnp.sum(p, axis=-1, keepdims=True) + jnp.exp(sink - m)
    o = jnp.dot((p / denom).astype(BF16), v, preferred_element_type=F32)
    for g in range(Q_PER_KV):
        o_ref[0, :, g * HEAD_DIM:(g + 1) * HEAD_DIM] = o[g * tq:(g + 1) * tq].astype(o_ref.dtype)


def local_attention(q, k, v, sink):
    B, _, S, _ = q.shape
    tq = _tile(S, 256, WINDOW)
    r = tq // WINDOW
    nb = S // WINDOW
    qspec = pl.BlockSpec((1, Q_PER_KV, tq, HEAD_DIM), lambda b, h, i: (b, h, i, 0))
    prev = pl.BlockSpec((1, 1, WINDOW, HEAD_DIM), lambda b, h, i: (b, h, jnp.maximum(i * r - 1, 0), 0))
    cur = pl.BlockSpec((1, 1, tq, HEAD_DIM), lambda b, h, i: (b, h, i, 0))
    nxt = pl.BlockSpec((1, 1, WINDOW, HEAD_DIM), lambda b, h, i: (b, h, jnp.minimum((i + 1) * r, nb - 1), 0))
    sink_rows = jnp.repeat(sink.astype(F32).reshape(N_KV_HEADS, Q_PER_KV), tq, axis=1)[:, :, None]
    return pl.pallas_call(
        functools.partial(_local_attn_kernel, tq=tq, seq=S),
        grid=(B, N_KV_HEADS, S // tq),
        in_specs=[qspec, prev, cur, nxt, prev, cur, nxt,
                  pl.BlockSpec((1, Q_PER_KV * tq, 1), lambda b, h, i: (h, 0, 0))],
        out_specs=pl.BlockSpec((1, tq, Q_PER_KV * HEAD_DIM), lambda b, h, i: (b, i, h)),
        out_shape=jax.ShapeDtypeStruct((B, S, BRANCH_W), BF16),
        compiler_params=_params("parallel", "parallel", "parallel"),
        name="local_attention",
    )(q, k, k, k, v, v, v, sink_rows)


def _global_attn_kernel(q_ref, k_ref, v_ref, o_ref, m_ref, l_ref, acc_ref, *, tq, nk):
    j = pl.program_id(3)

    @pl.when(j == 0)
    def _():
        m_ref[...] = jnp.full(m_ref.shape, NEG, F32)
        l_ref[...] = jnp.zeros(l_ref.shape, F32)
        acc_ref[...] = jnp.zeros(acc_ref.shape, F32)

    q = q_ref[0].reshape(Q_PER_KV * tq, HEAD_DIM)
    s = lax.dot_general(q, k_ref[0, 0], (((1,), (1,)), ((), ())), preferred_element_type=F32) * ATTN_SCALE
    m_prev = m_ref[...]
    m_new = jnp.maximum(m_prev, jnp.max(s, axis=-1, keepdims=True))
    a = jnp.exp(m_prev - m_new)
    p = jnp.exp(s - m_new)
    l_ref[...] = a * l_ref[...] + jnp.sum(p, axis=-1, keepdims=True)
    acc_ref[...] = a * acc_ref[...] + jnp.dot(p.astype(BF16), v_ref[0, 0], preferred_element_type=F32)
    m_ref[...] = m_new

    @pl.when(j == nk - 1)
    def _():
        o = acc_ref[...] / l_ref[...]
        for g in range(Q_PER_KV):
            o_ref[0, :, g * HEAD_DIM:(g + 1) * HEAD_DIM] = o[g * tq:(g + 1) * tq].astype(o_ref.dtype)


def global_attention(q, k, v):
    B, _, S, _ = q.shape
    tq = _tile(S, 512, 16)
    tk = _tile(S, 2048, LANE)
    nk = S // tk
    rows = Q_PER_KV * tq
    return pl.pallas_call(
        functools.partial(_global_attn_kernel, tq=tq, nk=nk),
        grid=(B, N_KV_HEADS, S // tq, nk),
        in_specs=[pl.BlockSpec((1, Q_PER_KV, tq, HEAD_DIM), lambda b, h, i, j: (b, h, i, 0)),
                  pl.BlockSpec((1, 1, tk, HEAD_DIM), lambda b, h, i, j: (b, h, j, 0)),
                  pl.BlockSpec((1, 1, tk, HEAD_DIM), lambda b, h, i, j: (b, h, j, 0))],
        out_specs=pl.BlockSpec((1, tq, Q_PER_KV * HEAD_DIM), lambda b, h, i, j: (b, i, h)),
        out_shape=jax.ShapeDtypeStruct((B, S, BRANCH_W), BF16),
        scratch_shapes=[pltpu.VMEM((rows, 1), F32), pltpu.VMEM((rows, 1), F32),
                        pltpu.VMEM((rows, HEAD_DIM), F32)],
        compiler_params=_params("parallel", "parallel", "parallel", "arbitrary"),
        name="global_attention",
    )(q, k, v)


def _dft_kernel(c_ref, s_ref, ab_ref, o_ref, acc_ref, *, nk):
    k = pl.program_id(2)
    part = (jnp.dot(c_ref[...], ab_ref[0, :, :BRANCH_W], preferred_element_type=F32)
            + jnp.dot(s_ref[...], ab_ref[0, :, BRANCH_W:], preferred_element_type=F32))

    @pl.when(k == 0)
    def _():
        acc_ref[...] = part

    @pl.when(k > 0)
    def _():
        acc_ref[...] += part

    @pl.when(k == nk - 1)
    def _():
        o_ref[0] = acc_ref[...].astype(o_ref.dtype)


def dft_positions(cos_s, neg_sin_s, ab):
    B, S, _ = ab.shape
    tm = _tile(S, 2048, 16)
    tk = _tile(S, 1024, LANE)
    nk = S // tk
    return pl.pallas_call(
        functools.partial(_dft_kernel, nk=nk),
        grid=(B, S // tm, nk),
        in_specs=[pl.BlockSpec((tm, tk), lambda b, i, k: (i, k)),
                  pl.BlockSpec((tm, tk), lambda b, i, k: (i, k)),
                  pl.BlockSpec((1, tk, 2 * BRANCH_W), lambda b, i, k: (b, k, 0))],
        out_specs=pl.BlockSpec((1, tm, BRANCH_W), lambda b, i, k: (b, i, 0)),
        out_shape=jax.ShapeDtypeStruct((B, S, BRANCH_W), BF16),
        scratch_shapes=[pltpu.VMEM((tm, BRANCH_W), F32)],
        compiler_params=_params("parallel", "parallel", "arbitrary"),
        name="dft_positions",
    )(cos_s, neg_sin_s, ab)


def _pool_kernel(pp_ref, pc_ref, pn_ref, band_ref, pw_ref, ps_ref, o_ref):
    def split(x):
        hi = x.astype(BF16)
        return hi, (x - hi.astype(F32)).astype(BF16)

    for g in range(N_GROUPS):
        sl = slice(g * GROUP_W, (g + 1) * GROUP_W)
        pc = pc_ref[0, :, sl]
        mean = jnp.zeros(pc.shape, F32)
        for j, ref in enumerate((pp_ref, pc_ref, pn_ref)):
            hi, lo = split(ref[0, :, sl])
            band = band_ref[g, 0, j]
            mean = mean + jnp.dot(band, hi, preferred_element_type=F32)
            mean = mean + jnp.dot(band, lo, preferred_element_type=F32)
        d = (mean - pc).astype(BF16)
        y = jnp.dot(d, pw_ref[g], preferred_element_type=F32) * ps_ref[:, sl]
        o_ref[0, :, sl] = y.astype(o_ref.dtype)


def pool_mix(z, band, pool_w, pool_scale):
    B, S, _ = z.shape
    T = band.shape[-1]
    nb = S // T
    return pl.pallas_call(
        _pool_kernel,
        grid=(B, nb),
        in_specs=[pl.BlockSpec((1, T, BRANCH_W), lambda b, i: (b, jnp.maximum(i - 1, 0), 1)),
                  pl.BlockSpec((1, T, BRANCH_W), lambda b, i: (b, i, 1)),
                  pl.BlockSpec((1, T, BRANCH_W), lambda b, i: (b, jnp.minimum(i + 1, nb - 1), 1)),
                  pl.BlockSpec((N_GROUPS, 1, 3, T, T), lambda b, i: (0, i, 0, 0, 0)),
                  pl.BlockSpec((N_GROUPS, GROUP_W, GROUP_W), lambda b, i: (0, 0, 0)),
                  pl.BlockSpec((1, BRANCH_W), lambda b, i: (0, 0))],
        out_specs=pl.BlockSpec((1, T, BRANCH_W), lambda b, i: (b, i, 0)),
        out_shape=jax.ShapeDtypeStruct((B, S, BRANCH_W), BF16),
        compiler_params=_params("parallel", "parallel"),
        name="pool_mix",
    )(z, z, z, band, pool_w, pool_scale.reshape(1, BRANCH_W).astype(F32))


def _merge_kernel(u_ref, wg_ref, bg_ref, o_ref, wb_ref, out_ref, acc_ref):
    bi = pl.program_id(2)
    gate = jax.nn.sigmoid(jnp.dot(u_ref[...], wg_ref[0], preferred_element_type=F32) + bg_ref[0])
    term = gate * jnp.dot(o_ref[0], wb_ref[0], preferred_element_type=F32)

    @pl.when(bi == 0)
    def _():
        acc_ref[...] = term

    @pl.when(jnp.logical_and(bi > 0, bi < N_BRANCH - 1))
    def _():
        acc_ref[...] += term

    @pl.when(bi == N_BRANCH - 1)
    def _():
        out_ref[...] = (acc_ref[...] + term).astype(out_ref.dtype)


def merge_branches(u, w_gate, b_gate, o_stack, w_branch, *, tm=1024, tn=512):
    M, D = u.shape
    tm = _tile(M, tm, SUBLANE)
    tn = _tile(D, tn, LANE)
    return pl.pallas_call(
        _merge_kernel,
        grid=(M // tm, D // tn, N_BRANCH),
        in_specs=[pl.BlockSpec((tm, D), lambda i, j, bi: (i, 0)),
                  pl.BlockSpec((1, D, tn), lambda i, j, bi: (bi, 0, j)),
                  pl.BlockSpec((1, 1, tn), lambda i, j, bi: (bi, 0, j)),
                  pl.BlockSpec((1, tm, BRANCH_W), lambda i, j, bi: (bi, i, 0)),
                  pl.BlockSpec((1, BRANCH_W, tn), lambda i, j, bi: (bi, 0, j))],
        out_specs=pl.BlockSpec((tm, tn), lambda i, j, bi: (i, j)),
        out_shape=jax.ShapeDtypeStruct((M, D), BF16),
        scratch_shapes=[pltpu.VMEM((tm, tn), F32)],
        compiler_params=_params("parallel", "parallel", "arbitrary"),
        name="merge_branches",
    )(u, w_gate, b_gate.reshape(N_BRANCH, 1, D).astype(F32), o_stack, w_branch)


def _rope(pos, dim):
    inv = ROPE_THETA ** (-jnp.arange(0, dim, 2, dtype=F32) / dim)
    ang = pos.astype(F32)[:, None] * inv[None, :]
    ang = jnp.concatenate([ang, ang], axis=-1)
    return jnp.cos(ang), jnp.sin(ang)


def make_tables(S):
    t = jnp.arange(S, dtype=jnp.int32)
    lane = jnp.arange(HEAD_DIM)
    half, quarter = HEAD_DIM // 2, HEAD_DIM // 4

    cos1, sin1 = _rope(t, HEAD_DIM)
    sin1 = sin1 * jnp.where(lane < half, -1.0, 1.0)[None, :]
    cr, sr = _rope(t // GRID_W, half)
    cc, sc = _rope(t % GRID_W, half)
    cos_ax = jnp.concatenate([cr, cc], axis=-1)
    sin_ax = jnp.concatenate([sr, sc], axis=-1)
    first = (lane % half) < quarter
    sin_ax_a = jnp.where(first[None, :], -sin_ax, 0.0)
    sin_ax_b = jnp.where(first[None, :], 0.0, sin_ax)

    jk = (t[:, None] * t[None, :]) % S
    ang = jk.astype(F32) * (2.0 * math.pi / S)
    cos_s = jnp.cos(ang).astype(BF16)
    neg_sin_s = (-jnp.sin(ang)).astype(BF16)

    c = jnp.arange(GROUP_W, dtype=jnp.int32)
    angc = ((c[:, None] * c[None, :]) % GROUP_W).astype(F32) * (2.0 * math.pi / GROUP_W)
    scale = 1.0 / math.sqrt(S * GROUP_W)
    eye = jnp.eye(N_GROUPS, dtype=F32)
    w_chan = jnp.concatenate([jnp.kron(eye, jnp.cos(angc) * scale),
                              jnp.kron(eye, jnp.sin(angc) * scale)], axis=1).astype(BF16)

    T = _tile(S, 256, LANE)
    nb = S // T
    rows = t.reshape(nb, 1, T, 1)
    cols = ((jnp.arange(nb)[:, None] - 1 + jnp.arange(3)[None, :]) * T)[:, :, None, None] \
        + jnp.arange(T)[None, None, None, :]
    bands = []
    for w in POOL_WINDOWS:
        lo = jnp.maximum(rows - w // 2, 0)
        hi = jnp.minimum(rows + w // 2 - 1, S - 1) + 1
        inside = (cols >= lo) & (cols < hi)
        bands.append(jnp.where(inside, 1.0 / (hi - lo).astype(F32), 0.0))
    band = jnp.stack(bands).astype(BF16)

    return dict(cos1=cos1, sin1=sin1, cos_ax=cos_ax, sin_ax_a=sin_ax_a, sin_ax_b=sin_ax_b,
                cos_s=cos_s, neg_sin_s=neg_sin_s, w_chan=w_chan, band=band)


def trunk(x, p):
    B, S, D = x.shape
    M = B * S
    depth = p["norm_ffa"].shape[0]
    tab = make_tables(S)
    x = x.reshape(M, D)

    def ffn(x, g, wg, wu, wd):
        h = ffn_up(rmsnorm(x, g, BF16), wg, wu)
        return matmul(h, wd, out_dtype=F32, res=x, alpha=0.5, tn=256)

    for l in range(depth):
        x = ffn(x, p["norm_ffa"][l], p["ffa_w_gate"][l], p["ffa_w_up"][l], p["ffa_w_down"][l])

        u = rmsnorm(x, p["norm_mix"][l], BF16)
        z = matmul(u, p["w_in"][l], out_dtype=F32, tn=1024)
        z3 = z.reshape(B, S, -1)
        ab = matmul(z, tab["w_chan"], out_dtype=BF16)
        o_a = dft_positions(tab["cos_s"], tab["neg_sin_s"], ab.reshape(B, S, -1))
        o_b = pool_mix(z3, tab["band"], p["pool_w"][l], p["pool_scale"][l])
        qc, kc, vc, qd, kd, vd = qkv_prep(z3, tab, p["qk_gain_q"][l], p["qk_gain_k"][l])
        o_c = local_attention(qc, kc, vc, p["attn_sink"][l])
        o_d = global_attention(qd, kd, vd)
        o_stack = jnp.stack([o_a, o_b, o_c, o_d]).reshape(N_BRANCH, M, BRANCH_W)
        merged = merge_branches(u, p["w_gate"][l], p["b_gate"][l], o_stack, p["w_branch"][l])
        x = matmul(merged, p["w_out"][l], out_dtype=F32, res=x, alpha=1.0)

        x = ffn(x, p["norm_ffb"][l], p["ffb_w_gate"][l], p["ffb_w_up"][l], p["ffb_w_down"][l])

    return rmsnorm(x, p["final_norm"], F32).reshape(B, S, D)


_MATMUL_WEIGHTS = ("ffa_w_gate", "ffa_w_up", "ffa_w_down", "w_in", "pool_w", "w_branch", "w_gate",
                   "w_out", "ffb_w_gate", "ffb_w_up", "ffb_w_down")


@jax.jit
def _forward(x_prompt, x_sample, p):
    p = {k: (v.astype(BF16) if k in _MATMUL_WEIGHTS else v) for k, v in p.items()}
    return trunk(x_prompt, p), trunk(x_sample, p)


def kernel(x_prompt, x_sample, norm_ffa, ffa_w_gate, ffa_w_up, ffa_w_down, norm_mix, w_in, pool_w, pool_scale,
           attn_sink, qk_gain_q, qk_gain_k, w_branch, w_gate, b_gate, w_out, norm_ffb, ffb_w_gate, ffb_w_up,
           ffb_w_down, final_norm):
    p = dict(norm_ffa=norm_ffa, ffa_w_gate=ffa_w_gate, ffa_w_up=ffa_w_up, ffa_w_down=ffa_w_down,
             norm_mix=norm_mix, w_in=w_in, pool_w=pool_w, pool_scale=pool_scale, attn_sink=attn_sink,
             qk_gain_q=qk_gain_q, qk_gain_k=qk_gain_k, w_branch=w_branch, w_gate=w_gate, b_gate=b_gate,
             w_out=w_out, norm_ffb=norm_ffb, ffb_w_gate=ffb_w_gate, ffb_w_up=ffb_w_up,
             ffb_w_down=ffb_w_down, final_norm=final_norm)
    return _forward(x_prompt, x_sample, p)
```
